```python
import math
import jax, jax.numpy as jnp
from jax import lax
import numpy as np

D_MODEL = 4096
BATCH = 4
SEQ = 2048
DEPTH = 1

CONV_CH = D_MODEL
CONV_WIDTH = 31
HEAD_DIM = 128
DIL_GROUPS = ((128, 1), (512, 4), (2048, 16))
HEADS_PER_GROUP = 8
N_ATTN_HEADS = HEADS_PER_GROUP * len(DIL_GROUPS)
ATTN_WIDTH = N_ATTN_HEADS * HEAD_DIM
ATTN_MIX_WIDTH = HEADS_PER_GROUP * HEAD_DIM
ATTN_BLOCK = 128
ALIBI_MAX_EXP = 8.0
N_BRANCH = 2
IN_WIDTH = 2 * CONV_CH + 3 * ATTN_WIDTH + N_BRANCH * D_MODEL
PEER_HEADS = 8
PEER_NKEYS = 128
PEER_EXPERTS = PEER_NKEYS * PEER_NKEYS
PEER_QDIM = 256
PEER_TOPK = 16
PEER_TOKEN_BLOCK = 64
RMS_EPS = 1e-6
LN_EPS = 1e-5

kernel_name = "hybrid_gated_conv_dilattn_peer"


def rms_norm(x, g):
    xf = x.astype(jnp.float32)
    y = xf * lax.rsqrt(jnp.mean(xf * xf, axis=-1, keepdims=True) + RMS_EPS)
    return (y * g.astype(jnp.float32)).astype(x.dtype)


def layer_norm(x, g, b):
    xf = x.astype(jnp.float32)
    mu = jnp.mean(xf, axis=-1, keepdims=True)
    var = jnp.mean(jnp.square(xf - mu), axis=-1, keepdims=True)
    y = (xf - mu) * lax.rsqrt(var + LN_EPS)
    return (y * g.astype(jnp.float32) + b.astype(jnp.float32)).astype(x.dtype)


def alibi_slopes():
    h = jnp.arange(1, N_ATTN_HEADS + 1, dtype=jnp.float32)
    return jnp.exp2(-ALIBI_MAX_EXP * h / N_ATTN_HEADS)


def conformer_conv(glu_in, dw_w, dw_b, ln_g, ln_b, pw_w):
    a, b = jnp.split(glu_in, 2, axis=-1)
    h = a * jax.nn.sigmoid(b)
    h = lax.conv_general_dilated(
        h, dw_w[:, None, :].astype(h.dtype), window_strides=(1,),
        padding=[(CONV_WIDTH - 1, 0)],
        dimension_numbers=("NWC", "WIO", "NWC"),
        feature_group_count=CONV_CH) + dw_b
    h = jax.nn.silu(layer_norm(h, ln_g, ln_b))
    return h @ pw_w


def dilated_group_attention(q, k, v, slopes, window, dilation):
    B, S, H, Dh = q.shape
    L = S // dilation
    W = window // dilation
    assert W <= ATTN_BLOCK
    nb = -(-L // ATTN_BLOCK)
    Lp = nb * ATTN_BLOCK

    def strided_blocks(t):
        t = t.reshape(B, L, dilation, H, Dh).transpose(0, 2, 3, 1, 4)
        t = jnp.pad(t, ((0, 0), (0, 0), (0, 0), (0, Lp - L), (0, 0)))
        return t.reshape(B, dilation, H, nb, ATTN_BLOCK, Dh)

    def with_prev(t):
        prev = jnp.pad(t, ((0, 0), (0, 0), (0, 0), (1, 0), (0, 0), (0, 0)))[:, :, :, :-1]
        return jnp.concatenate([prev, t], axis=-2)

    qb = strided_blocks(q)
    kk = with_prev(strided_blocks(k))
    vv = with_prev(strided_blocks(v))

    s = jnp.einsum("bdhnqc,bdhnkc->bdhnqk", qb, kk).astype(jnp.float32) * (Dh ** -0.5)
    qi = jnp.arange(ATTN_BLOCK)[:, None]
    ki = jnp.arange(2 * ATTN_BLOCK)[None, :]
    delta = qi + ATTN_BLOCK - ki
    key_idx = jnp.arange(nb)[:, None, None] * ATTN_BLOCK - ATTN_BLOCK + ki[None]
    valid = (delta >= 0) & (delta <= W) & (key_idx >= 0)
    dist = (delta * dilation).astype(jnp.float32)
    bias = -slopes.astype(jnp.float32)[:, None, None, None] * dist
    s = jnp.where(valid, s + bias, -jnp.inf)
    m = jnp.max(s, axis=-1, keepdims=True)
    p = jnp.exp(s - m)
    den = jnp.sum(p, axis=-1, keepdims=True)
    o = jnp.einsum("bdhnqk,bdhnkc->bdhnqc", p, vv.astype(jnp.float32)) / den
    lse = (m + jnp.log(den))[..., 0]

    o = o.reshape(B, dilation, H, Lp, Dh)[:, :, :, :L].transpose(0, 3, 1, 2, 4).reshape(B, S, H, Dh)
    lse = lse.reshape(B, dilation, H, Lp)[:, :, :, :L].transpose(0, 3, 1, 2).reshape(B, S, H)
    return o, lse


def dilated_mixture_attention(q, k, v):
    B, S = q.shape[:2]
    slopes = alibi_slopes()
    outs, lses = [], []
    for gi, (window, dilation) in enumerate(DIL_GROUPS):
        sl = slice(gi * HEADS_PER_GROUP, (gi + 1) * HEADS_PER_GROUP)
        o, lse = dilated_group_attention(q[:, :, sl], k[:, :, sl], v[:, :, sl],
                                         slopes[sl], window, dilation)
        outs.append(o)
        lses.append(lse)
    alpha = jax.nn.softmax(jnp.stack(lses, axis=0), axis=0)
    mixed = jnp.sum(alpha[..., None] * jnp.stack(outs, axis=0), axis=0)
    return mixed.reshape(B, S, ATTN_MIX_WIDTH).astype(q.dtype)


def peer_ffn(x, w_q, sub_keys_1, sub_keys_2, u_emb, v_emb):
    B, S, D = x.shape
    T = B * S
    xt = x.reshape(T, D)
    q = (xt @ w_q).reshape(T, PEER_HEADS, 2, PEER_QDIM // 2)
    s1 = jnp.einsum("thc,kc->thk", q[:, :, 0], sub_keys_1)
    s2 = jnp.einsum("thc,kc->thk", q[:, :, 1], sub_keys_2)
    v1, i1 = lax.top_k(s1, PEER_TOPK)
    v2, i2 = lax.top_k(s2, PEER_TOPK)
    cand = (v1[..., :, None] + v2[..., None, :]).reshape(T, PEER_HEADS, PEER_TOPK * PEER_TOPK)
    sc, ci = lax.top_k(cand, PEER_TOPK)
    e1 = jnp.take_along_axis(i1, ci // PEER_TOPK, axis=-1)
    e2 = jnp.take_along_axis(i2, ci % PEER_TOPK, axis=-1)
    idx = (e1 * PEER_NKEYS + e2).astype(jnp.int32)
    g = jax.nn.softmax(sc.astype(jnp.float32), axis=-1)

    def expert_block(args):
        xb, ib, gb = args
        u = u_emb[ib]
        a = jax.nn.gelu(jnp.einsum("td,thkd->thk", xb, u).astype(jnp.float32), approximate=False)
        w = (gb * a).astype(xb.dtype)
        return jnp.einsum("thk,thkd->td", w, v_emb[ib])

    nblk = T // PEER_TOKEN_BLOCK
    out = lax.map(expert_block, (xt.reshape(nblk, PEER_TOKEN_BLOCK, D),
                                 idx.reshape(nblk, PEER_TOKEN_BLOCK, PEER_HEADS, PEER_TOPK),
                                 g.reshape(nblk, PEER_TOKEN_BLOCK, PEER_HEADS, PEER_TOPK)))
    return out.reshape(B, S, D)


def setup_inputs(seed: int = 0) -> dict:
    key = jax.random.key(seed)
    ks = jax.random.split(key, 20)
    f32 = jnp.float32
    nrm = lambda k, shape, scale: jax.random.normal(k, shape, f32) * scale
    gain = lambda k, shape: 1.0 + 0.01 * jax.random.normal(k, shape, f32)
    return {
        "x": jax.random.normal(ks[0], (BATCH, SEQ, D_MODEL), f32),
        "norm_mix_g": gain(ks[1], (DEPTH, D_MODEL)),
        "w_in": nrm(ks[2], (DEPTH, D_MODEL, IN_WIDTH), D_MODEL ** -0.5),
        "gate_b": nrm(ks[3], (DEPTH, N_BRANCH * D_MODEL), 0.01),
        "conv_dw_w": nrm(ks[4], (DEPTH, CONV_WIDTH, CONV_CH), CONV_WIDTH ** -0.5),
        "conv_dw_b": nrm(ks[5], (DEPTH, CONV_CH), 0.01),
        "conv_ln_g": gain(ks[6], (DEPTH, CONV_CH)),
        "conv_ln_b": nrm(ks[7], (DEPTH, CONV_CH), 0.01),
        "conv_pw_w": nrm(ks[8], (DEPTH, CONV_CH, D_MODEL), CONV_CH ** -0.5),
        "attn_o_w": nrm(ks[9], (DEPTH, ATTN_MIX_WIDTH, D_MODEL), ATTN_MIX_WIDTH ** -0.5),
        "w_out": nrm(ks[10], (DEPTH, D_MODEL, D_MODEL), D_MODEL ** -0.5),
        "norm_ffn_g": gain(ks[11], (DEPTH, D_MODEL)),
        "peer_w_q": nrm(ks[12], (DEPTH, D_MODEL, PEER_HEADS * PEER_QDIM), D_MODEL ** -0.5),
        "peer_sub_keys_1": nrm(ks[13], (DEPTH, PEER_NKEYS, PEER_QDIM // 2), (PEER_QDIM // 2) ** -0.5),
        "peer_sub_keys_2": nrm(ks[14], (DEPTH, PEER_NKEYS, PEER_QDIM // 2), (PEER_QDIM // 2) ** -0.5),
        "peer_u": nrm(ks[15], (DEPTH, PEER_EXPERTS, D_MODEL), D_MODEL ** -0.5),
        "peer_v": nrm(ks[16], (DEPTH, PEER_EXPERTS, D_MODEL), PEER_HEADS ** -0.5),
        "norm_final_g": gain(ks[17], (D_MODEL,)),
    }


def reference(x, norm_mix_g, w_in, gate_b, conv_dw_w, conv_dw_b, conv_ln_g, conv_ln_b,
              conv_pw_w, attn_o_w, w_out, norm_ffn_g, peer_w_q, peer_sub_keys_1,
              peer_sub_keys_2, peer_u, peer_v, norm_final_g):
    B, S, D = x.shape
    o_glu = 2 * CONV_CH
    o_q = o_glu + ATTN_WIDTH
    o_k = o_q + ATTN_WIDTH
    o_v = o_k + ATTN_WIDTH
    h = x
    for l in range(DEPTH):
        xn = rms_norm(h, norm_mix_g[l])
        proj = xn @ w_in[l]
        conv_out = conformer_conv(proj[..., :o_glu], conv_dw_w[l], conv_dw_b[l],
                                  conv_ln_g[l], conv_ln_b[l], conv_pw_w[l])
        q = proj[..., o_glu:o_q].reshape(B, S, N_ATTN_HEADS, HEAD_DIM)
        k = proj[..., o_q:o_k].reshape(B, S, N_ATTN_HEADS, HEAD_DIM)
        v = proj[..., o_k:o_v].reshape(B, S, N_ATTN_HEADS, HEAD_DIM)
        attn_out = dilated_mixture_attention(q, k, v) @ attn_o_w[l]
        gates = jax.nn.sigmoid(proj[..., o_v:] + gate_b[l]).reshape(B, S, N_BRANCH, D)
        mixed = gates[:, :, 0] * conv_out + gates[:, :, 1] * attn_out
        h = h + mixed @ w_out[l]
        hn = rms_norm(h, norm_ffn_g[l])
        h = h + peer_ffn(hn, peer_w_q[l], peer_sub_keys_1[l], peer_sub_keys_2[l],
                         peer_u[l], peer_v[l])
    return rms_norm(h, norm_final_g)
```

```python
import functools

import jax
import jax.numpy as jnp
from jax import lax
from jax.experimental import pallas as pl
from jax.experimental.pallas import tpu as pltpu

F32 = jnp.float32
BF16 = jnp.bfloat16

D_MODEL = 4096
CONV_CH = D_MODEL
CONV_WIDTH = 31
HEAD_DIM = 128
DIL_GROUPS = ((128, 1), (512, 4), (2048, 16))
HEADS_PER_GROUP = 8
N_ATTN_HEADS = HEADS_PER_GROUP * len(DIL_GROUPS)
ATTN_WIDTH = N_ATTN_HEADS * HEAD_DIM
ATTN_MIX_WIDTH = HEADS_PER_GROUP * HEAD_DIM
ATTN_BLOCK = 128
ALIBI_MAX_EXP = 8.0
PEER_HEADS = 8
PEER_NKEYS = 128
PEER_EXPERTS = PEER_NKEYS * PEER_NKEYS
PEER_QDIM = 256
PEER_TOPK = 16
RMS_EPS = 1e-6
LN_EPS = 1e-5

V7X_LANES = 128
V7X_SUBLANES = 8
V7X_VMEM_LIMIT_BYTES = 56 * 1024 * 1024

NEG_BIG = -1e30
INV_SQRT2 = 0.7071067811865476


def _params(*sem):
    return pltpu.CompilerParams(
        dimension_semantics=sem if sem else None,
        vmem_limit_bytes=V7X_VMEM_LIMIT_BYTES)


def _rmsnorm_body(x_ref, g_ref, o_ref):
    x = x_ref[...]
    ms = jnp.mean(x * x, axis=-1, keepdims=True)
    o_ref[...] = (x * lax.rsqrt(ms + RMS_EPS) * g_ref[...]).astype(o_ref.dtype)


def _rmsnorm(x2d, g, out_dtype, tm=256):
    m, d = x2d.shape
    return pl.pallas_call(
        _rmsnorm_body,
        grid=(m // tm,),
        in_specs=[pl.BlockSpec((tm, d), lambda i: (i, 0)),
                  pl.BlockSpec((1, d), lambda i: (0, 0))],
        out_specs=pl.BlockSpec((tm, d), lambda i: (i, 0)),
        out_shape=jax.ShapeDtypeStruct((m, d), out_dtype),
        compiler_params=_params("arbitrary"),
        name="rmsnorm",
    )(x2d, g.reshape(1, d))


def _glu_body(a_ref, wa_ref, wb_ref, o_ref):
    a = a_ref[...]
    ya = jnp.dot(a, wa_ref[...], preferred_element_type=F32)
    yb = jnp.dot(a, wb_ref[...], preferred_element_type=F32)
    o_ref[...] = (ya * jax.nn.sigmoid(yb)).astype(o_ref.dtype)


def _plain_body(a_ref, w_ref, o_ref):
    o_ref[...] = jnp.dot(a_ref[...], w_ref[...],
                         preferred_element_type=F32).astype(o_ref.dtype)


def _gate_body(a_ref, w_ref, b_ref, o_ref):
    y = jnp.dot(a_ref[...], w_ref[...], preferred_element_type=F32)
    o_ref[...] = jax.nn.sigmoid(y + b_ref[...]).astype(o_ref.dtype)


def _proj_glu(xn, w, tm=512, tn=512):
    m, k = xn.shape
    n = CONV_CH
    off = n // tn
    return pl.pallas_call(
        _glu_body,
        grid=(n // tn, m // tm),
        in_specs=[pl.BlockSpec((tm, k), lambda j, i: (i, 0)),
                  pl.BlockSpec((k, tn), lambda j, i: (0, j)),
                  pl.BlockSpec((k, tn), lambda j, i: (0, j + off))],
        out_specs=pl.BlockSpec((tm, tn), lambda j, i: (i, j)),
        out_shape=jax.ShapeDtypeStruct((m, n), F32),
        compiler_params=_params("arbitrary", "arbitrary"),
        name="proj_glu",
    )(xn, w, w)


def _proj_plain(a, w, col0, n, out_dtype, tm=512, tn=1024, name="proj"):
    m, k = a.shape
    off = col0 // tn
    return pl.pallas_call(
        _plain_body,
        grid=(n // tn, m // tm),
        in_specs=[pl.BlockSpec((tm, k), lambda j, i: (i, 0)),
                  pl.BlockSpec((k, tn), lambda j, i: (0, j + off))],
        out_specs=pl.BlockSpec((tm, tn), lambda j, i: (i, j)),
        out_shape=jax.ShapeDtypeStruct((m, n), out_dtype),
        compiler_params=_params("arbitrary", "arbitrary"),
        name=name,
    )(a, w)


def _proj_gate(xn, w, bias, col0, tm=512, tn=1024):
    m, k = xn.shape
    n = bias.shape[-1]
    off = col0 // tn
    return pl.pallas_call(
        _gate_body,
        grid=(n // tn, m // tm),
        in_specs=[pl.BlockSpec((tm, k), lambda j, i: (i, 0)),
                  pl.BlockSpec((k, tn), lambda j, i: (0, j + off)),
                  pl.BlockSpec((1, tn), lambda j, i: (0, j))],
        out_specs=pl.BlockSpec((tm, tn), lambda j, i: (i, j)),
        out_shape=jax.ShapeDtypeStruct((m, n), BF16),
        compiler_params=_params("arbitrary", "arbitrary"),
        name="proj_gate",
    )(xn, w, bias.reshape(1, n))


CONV_HALO = 32
CONV_ROWS = 64


def _conv_body(h_ref, halo_ref, w_ref, b_ref, g_ref, be_ref, o_ref, hp_ref, cv_ref,
               *, ts, cw):
    i = pl.program_id(1)
    c = pl.program_id(2)
    nc = pl.num_programs(2)
    halo = halo_ref[0]
    hp_ref[0:CONV_HALO, :] = jnp.where(i > 0, halo, jnp.zeros_like(halo))
    hp_ref[CONV_HALO:, :] = h_ref[0]
    base = CONV_HALO - (CONV_WIDTH - 1)
    for r0 in range(0, ts, CONV_ROWS):
        acc = jnp.zeros((CONV_ROWS, cw), F32) + b_ref[...]
        for w in range(CONV_WIDTH):
            acc = acc + hp_ref[r0 + base + w:r0 + base + w + CONV_ROWS, :] * w_ref[w:w + 1, :]
        cv_ref[c, r0:r0 + CONV_ROWS, :] = acc

    @pl.when(c == nc - 1)
    def _():
        n_ch = cv_ref.shape[0]
        tot = jnp.zeros((ts, 1), F32)
        for k in range(n_ch):
            tot = tot + jnp.sum(cv_ref[k], axis=-1, keepdims=True)
        mu = tot * (1.0 / (n_ch * cw))
        var = jnp.zeros((ts, 1), F32)
        for k in range(n_ch):
            d = cv_ref[k] - mu
            var = var + jnp.sum(d * d, axis=-1, keepdims=True)
        rstd = lax.rsqrt(var * (1.0 / (n_ch * cw)) + LN_EPS)
        for k in range(n_ch):
            y = (cv_ref[k] - mu) * rstd * g_ref[:, k * cw:(k + 1) * cw] + be_ref[:, k * cw:(k + 1) * cw]
            o_ref[0, :, k * cw:(k + 1) * cw] = (y * jax.nn.sigmoid(y)).astype(o_ref.dtype)


def _conformer_conv(h3, dw_w, dw_b, ln_g, ln_b, ts=256, cw=512):
    b, s, ch = h3.shape
    hb = ts // CONV_HALO
    body = functools.partial(_conv_body, ts=ts, cw=cw)
    return pl.pallas_call(
        body,
        grid=(b, s // ts, ch // cw),
        in_specs=[
            pl.BlockSpec((1, ts, cw), lambda bb, i, c: (bb, i, c)),
            pl.BlockSpec((1, CONV_HALO, cw), lambda bb, i, c: (bb, jnp.maximum(i * hb - 1, 0), c)),
            pl.BlockSpec((CONV_WIDTH, cw), lambda bb, i, c: (0, c)),
            pl.BlockSpec((1, cw), lambda bb, i, c: (0, c)),
            pl.BlockSpec((1, ch), lambda bb, i, c: (0, 0)),
            pl.BlockSpec((1, ch), lambda bb, i, c: (0, 0)),
        ],
        out_specs=pl.BlockSpec((1, ts, ch), lambda bb, i, c: (bb, i, 0)),
        out_shape=jax.ShapeDtypeStruct((b, s, ch), BF16),
        scratch_shapes=[pltpu.VMEM((ts + CONV_HALO, cw), F32),
                        pltpu.VMEM((ch // cw, ts, cw), F32)],
        compiler_params=_params("arbitrary", "arbitrary", "arbitrary"),
        name="conformer_conv",
    )(h3, h3, dw_w, dw_b.reshape(1, ch), ln_g.reshape(1, ch), ln_b.reshape(1, ch))


def _attn_body(q_ref, kp_ref, kc_ref, vp_ref, vc_ref, sl_ref, o_ref, lse_ref,
               *, dilation, win):
    n = pl.program_id(2)
    blk = ATTN_BLOCK
    qi = lax.broadcasted_iota(jnp.int32, (blk, blk), 0)
    kj = lax.broadcasted_iota(jnp.int32, (blk, blk), 1)
    delta_c = qi - kj
    delta_p = qi + blk - kj
    valid_c = delta_c >= 0
    if win < blk:
        valid_c = valid_c & (delta_c <= win)
    valid_p = delta_p <= win
    dist_c = (delta_c * dilation).astype(F32)
    dist_p = (delta_p * dilation).astype(F32)
    scale = HEAD_DIM ** -0.5
    nt = (((1,), (1,)), ((), ()))
    has_prev = n > 0
    for h in range(HEADS_PER_GROUP):
        cs = slice(h * HEAD_DIM, (h + 1) * HEAD_DIM)
        q = q_ref[0, :, cs]
        slope = sl_ref[:, cs]
        s_c = lax.dot_general(q, kc_ref[0, :, cs], nt, preferred_element_type=F32) * scale
        s_p = lax.dot_general(q, kp_ref[0, :, cs], nt, preferred_element_type=F32) * scale
        s_c = jnp.where(valid_c, s_c + (-slope) * dist_c, NEG_BIG)
        s_p = jnp.where(valid_p, s_p + (-slope) * dist_p, NEG_BIG)
        s_p = jnp.where(has_prev, s_p, NEG_BIG)
        m = jnp.maximum(jnp.max(s_c, axis=-1, keepdims=True),
                        jnp.max(s_p, axis=-1, keepdims=True))
        p_c = jnp.exp(s_c - m)
        p_p = jnp.exp(s_p - m)
        den = jnp.sum(p_c, axis=-1, keepdims=True) + jnp.sum(p_p, axis=-1, keepdims=True)
        o = (jnp.dot(p_c.astype(BF16), vc_ref[0, :, cs], preferred_element_type=F32)
             + jnp.dot(p_p.astype(BF16), vp_ref[0, :, cs], preferred_element_type=F32))
        o_ref[0, :, cs] = o / den
        lse_ref[0, :, cs] = jnp.broadcast_to(m + jnp.log(den), (blk, HEAD_DIM))


def _attn_group(qkv3, slopes, gi, window, dilation):
    b, s, width = qkv3.shape
    l = s // dilation
    win = window // dilation
    assert l % ATTN_BLOCK == 0 and win <= ATTN_BLOCK
    nb = l // ATTN_BLOCK
    gw = ATTN_MIX_WIDTH
    ncol = width // gw
    qv = qkv3.reshape(b, l, dilation * width)
    sl_row = jnp.repeat(slopes, HEAD_DIM).reshape(1, gw)

    def spec(which, prev):
        if prev:
            return pl.BlockSpec((1, ATTN_BLOCK, gw),
                                lambda bb, r, n: (bb, jnp.maximum(n - 1, 0), r * ncol + which * 3 + gi))
        return pl.BlockSpec((1, ATTN_BLOCK, gw),
                            lambda bb, r, n: (bb, n, r * ncol + which * 3 + gi))

    out_spec = pl.BlockSpec((1, ATTN_BLOCK, gw), lambda bb, r, n: (bb, n, r))
    o, lse = pl.pallas_call(
        functools.partial(_attn_body, dilation=dilation, win=win),
        grid=(b, dilation, nb),
        in_specs=[spec(0, False), spec(1, True), spec(1, False), spec(2, True), spec(2, False),
                  pl.BlockSpec((1, gw), lambda bb, r, n: (0, 0))],
        out_specs=[out_spec, out_spec],
        out_shape=[jax.ShapeDtypeStruct((b, l, dilation * gw), F32)] * 2,
        compiler_params=_params("arbitrary", "arbitrary", "arbitrary"),
        name=f"dil_attn_g{gi}",
    )(qv, qv, qv, qv, qv, sl_row)
    return o.reshape(b * s, gw), lse.reshape(b * s, gw)


def _mix_groups_body(o0, o1, o2, l0, l1, l2, out_ref):
    a0, a1, a2 = l0[...], l1[...], l2[...]
    m = jnp.maximum(jnp.maximum(a0, a1), a2)
    e0, e1, e2 = jnp.exp(a0 - m), jnp.exp(a1 - m), jnp.exp(a2 - m)
    den = e0 + e1 + e2
    out_ref[...] = ((e0 * o0[...] + e1 * o1[...] + e2 * o2[...]) / den).astype(out_ref.dtype)


def _mix_groups(outs, lses, tm=512):
    m, w = outs[0].shape
    spec = pl.BlockSpec((tm, w), lambda i: (i, 0))
    return pl.pallas_call(
        _mix_groups_body,
        grid=(m // tm,),
        in_specs=[spec] * 6,
        out_specs=spec,
        out_shape=jax.ShapeDtypeStruct((m, w), BF16),
        compiler_params=_params("arbitrary"),
        name="attn_group_mix",
    )(*outs, *lses)


def _merge_body(hc_ref, am_ref, pw_ref, ow_ref, g0_ref, g1_ref, o_ref):
    c = jnp.dot(hc_ref[...], pw_ref[...], preferred_element_type=F32)
    a = jnp.dot(am_ref[...], ow_ref[...], preferred_element_type=F32)
    o_ref[...] = (g0_ref[...].astype(F32) * c + g1_ref[...].astype(F32) * a).astype(o_ref.dtype)


def _gated_merge(hc, am, pw, ow, gates, tm=512, tn=512):
    m, k1 = hc.shape
    k2 = am.shape[1]
    n = pw.shape[1]
    off = n // tn
    return pl.pallas_call(
        _merge_body,
        grid=(n // tn, m // tm),
        in_specs=[pl.BlockSpec((tm, k1), lambda j, i: (i, 0)),
                  pl.BlockSpec((tm, k2), lambda j, i: (i, 0)),
                  pl.BlockSpec((k1, tn), lambda j, i: (0, j)),
                  pl.BlockSpec((k2, tn), lambda j, i: (0, j)),
                  pl.BlockSpec((tm, tn), lambda j, i: (i, j)),
                  pl.BlockSpec((tm, tn), lambda j, i: (i, j + off))],
        out_specs=pl.BlockSpec((tm, tn), lambda j, i: (i, j)),
        out_shape=jax.ShapeDtypeStruct((m, n), BF16),
        compiler_params=_params("arbitrary", "arbitrary"),
        name="gated_merge",
    )(hc, am, pw, ow, gates, gates)


def _outproj_body(a_ref, w_ref, x_ref, g_ref, h_ref, hn_ref, row_ref, *, tn):
    j = pl.program_id(1)
    nj = pl.num_programs(1)
    y = x_ref[...] + jnp.dot(a_ref[...], w_ref[...], preferred_element_type=F32)
    h_ref[...] = y
    row_ref[j] = y

    @pl.when(j == nj - 1)
    def _():
        n_chunks = row_ref.shape[0]
        ss = jnp.zeros((row_ref.shape[1], 1), F32)
        for k in range(n_chunks):
            v = row_ref[k]
            ss = ss + jnp.sum(v * v, axis=-1, keepdims=True)
        r = lax.rsqrt(ss * (1.0 / (n_chunks * tn)) + RMS_EPS)
        for k in range(n_chunks):
            hn_ref[:, k * tn:(k + 1) * tn] = (
                row_ref[k] * r * g_ref[:, k * tn:(k + 1) * tn]).astype(hn_ref.dtype)


def _out_proj(mixed, w, x2d, g, tm=512, tn=512):
    m, k = mixed.shape
    n = w.shape[1]
    return pl.pallas_call(
        functools.partial(_outproj_body, tn=tn),
        grid=(m // tm, n // tn),
        in_specs=[pl.BlockSpec((tm, k), lambda i, j: (i, 0)),
                  pl.BlockSpec((k, tn), lambda i, j: (0, j)),
                  pl.BlockSpec((tm, tn), lambda i, j: (i, j)),
                  pl.BlockSpec((1, n), lambda i, j: (0, 0))],
        out_specs=[pl.BlockSpec((tm, tn), lambda i, j: (i, j)),
                   pl.BlockSpec((tm, n), lambda i, j: (i, 0))],
        out_shape=[jax.ShapeDtypeStruct((m, n), F32),
                   jax.ShapeDtypeStruct((m, n), BF16)],
        scratch_shapes=[pltpu.VMEM((n // tn, tm, tn), F32)],
        compiler_params=_params("arbitrary", "arbitrary"),
        name="out_proj",
    )(mixed, w, x2d, g.reshape(1, n))


def _top16_rows(x):
    rid = lax.broadcasted_iota(jnp.int32, (PEER_TOPK, x.shape[1]), 0)
    vals = jnp.zeros((PEER_TOPK, x.shape[1]), F32)
    for r in range(PEER_TOPK):
        m = jnp.max(x, axis=0, keepdims=True)
        vals = jnp.where(rid == r, m, vals)
        x = jnp.where(x == m, -jnp.inf, x)
    return vals


def _peer_score_body(q_ref, k1_ref, k2_ref, p1_ref, e1_ref, p2_ref, e2_ref, tau_ref):
    tq = q_ref.shape[0]
    half = PEER_QDIM // 2
    nt = (((1,), (1,)), ((), ()))
    for h in range(PEER_HEADS):
        q1 = q_ref[:, h * PEER_QDIM:h * PEER_QDIM + half]
        q2 = q_ref[:, h * PEER_QDIM + half:(h + 1) * PEER_QDIM]
        p1_ref[h] = lax.dot_general(k1_ref[...], q1, nt, preferred_element_type=F32)
        p2_ref[h] = lax.dot_general(k2_ref[...], q2, nt, preferred_element_type=F32)
        for t0 in range(0, tq, V7X_LANES):
            ls = slice(t0, t0 + V7X_LANES)
            s1 = p1_ref[h, :, ls]
            s2 = p2_ref[h, :, ls]
            v1 = _top16_rows(s1)
            v2 = _top16_rows(s2)
            blocks = [v1[0:1] + v2]
            for a in range(1, 8):
                blocks.append(v1[a:a + 1] + v2[0:8])
            blocks.append(v1[8:16] + v2[0:1])
            cand = jnp.concatenate(blocks, axis=0)
            c = cand
            for _ in range(PEER_TOPK):
                tau = jnp.max(c, axis=0, keepdims=True)
                c = jnp.where(c == tau, -jnp.inf, c)
            m1 = v1[0:1]
            m2 = v2[0:1]
            z = jnp.sum(jnp.where(cand >= tau, jnp.exp(cand - (m1 + m2)), 0.0),
                        axis=0, keepdims=True)
            e1_ref[h, :, ls] = jnp.exp(s1 - m1) / z
            e2_ref[h, :, ls] = jnp.exp(s2 - m2)
            tau_ref[h:h + 1, ls] = tau


def _peer_scores(q, k1, k2, tq=256):
    t = q.shape[0]
    nk = PEER_NKEYS
    big = pl.BlockSpec((PEER_HEADS, nk, tq), lambda i: (0, 0, i))
    big_shape = jax.ShapeDtypeStruct((PEER_HEADS, nk, t), F32)
    return pl.pallas_call(
        _peer_score_body,
        grid=(t // tq,),
        in_specs=[pl.BlockSpec((tq, PEER_HEADS * PEER_QDIM), lambda i: (i, 0)),
                  pl.BlockSpec((nk, PEER_QDIM // 2), lambda i: (0, 0)),
                  pl.BlockSpec((nk, PEER_QDIM // 2), lambda i: (0, 0))],
        out_specs=[big, big, big, big, pl.BlockSpec((PEER_HEADS, tq), lambda i: (0, i))],
        out_shape=[big_shape] * 4 + [jax.ShapeDtypeStruct((PEER_HEADS, t), F32)],
        compiler_params=_params("arbitrary"),
        name="peer_scores",
    )(q, k1, k2)


PEER_SUB_E = 64
PEER_SUB_T = 128


def _peer_dense_body(hn_ref, ut_ref, v_ref, p1_ref, e1_ref, p2_ref, e2_ref, tau_ref,
                     o_ref, wt_ref, *, tm, te):
    j = pl.program_id(1)

    @pl.when(j == 0)
    def _():
        o_ref[...] = jnp.zeros_like(o_ref)

    s = jnp.dot(hn_ref[...], ut_ref[...], preferred_element_type=F32)
    act = 0.5 * s * (1.0 + lax.erf(s * INV_SQRT2))

    n_c = te // PEER_NKEYS
    assert 2 * n_c == V7X_SUBLANES
    base = pl.multiple_of((j // 2) * V7X_SUBLANES, V7X_SUBLANES)
    upper = (j % 2) == 1
    for t0 in range(0, tm, PEER_SUB_T):
        ls = slice(t0, t0 + PEER_SUB_T)
        rows = []
        for h in range(PEER_HEADS):
            g1 = p1_ref[h, pl.ds(base, V7X_SUBLANES), ls]
            ge = e1_ref[h, pl.ds(base, V7X_SUBLANES), ls]
            rows.append((jnp.where(upper, g1[n_c:], g1[:n_c]),
                         jnp.where(upper, ge[n_c:], ge[:n_c]),
                         tau_ref[h:h + 1, ls]))
        for cc in range(n_c):
            for r0 in range(0, PEER_NKEYS, PEER_SUB_E):
                rs = slice(r0, r0 + PEER_SUB_E)
                w = jnp.zeros((PEER_SUB_E, PEER_SUB_T), F32)
                for h in range(PEER_HEADS):
                    s1, e1, tau = rows[h]
                    tot = p2_ref[h, rs, ls] + s1[cc:cc + 1]
                    w = w + jnp.where(tot >= tau, e2_ref[h, rs, ls] * e1[cc:cc + 1], 0.0)
                wt_ref[cc * PEER_NKEYS + r0:cc * PEER_NKEYS + r0 + PEER_SUB_E, ls] = w

    w_te = wt_ref[...].T
    o_ref[...] += jnp.dot((w_te * act).astype(BF16), v_ref[...], preferred_element_type=F32)


def _peer_dense(hn, ut, v, p1, e1, p2, e2, tau, tm=512, te=512):
    t, d = hn.shape
    n_e = ut.shape[1]
    once = pl.Buffered(1)
    big = pl.BlockSpec((PEER_HEADS, PEER_NKEYS, tm), lambda i, j: (0, 0, i), pipeline_mode=once)
    return pl.pallas_call(
        functools.partial(_peer_dense_body, tm=tm, te=te),
        grid=(t // tm, n_e // te),
        in_specs=[pl.BlockSpec((tm, d), lambda i, j: (i, 0), pipeline_mode=once),
                  pl.BlockSpec((d, te), lambda i, j: (0, j)),
                  pl.BlockSpec((te, d), lambda i, j: (j, 0)),
                  big, big, big, big,
                  pl.BlockSpec((PEER_HEADS, tm), lambda i, j: (0, i), pipeline_mode=once)],
        out_specs=pl.BlockSpec((tm, d), lambda i, j: (i, 0)),
        out_shape=jax.ShapeDtypeStruct((t, d), F32),
        scratch_shapes=[pltpu.VMEM((te, tm), F32)],
        compiler_params=_params("arbitrary", "arbitrary"),
        name="peer_dense",
    )(hn, ut, v, p1, e1, p2, e2, tau)


def _final_body(h_ref, p_ref, g_ref, o_ref):
    x = h_ref[...] + p_ref[...]
    ms = jnp.mean(x * x, axis=-1, keepdims=True)
    o_ref[...] = x * lax.rsqrt(ms + RMS_EPS) * g_ref[...]


def _final_norm(h1, peer, g, tm=256):
    m, d = h1.shape
    spec = pl.BlockSpec((tm, d), lambda i: (i, 0))
    return pl.pallas_call(
        _final_body,
        grid=(m // tm,),
        in_specs=[spec, spec, pl.BlockSpec((1, d), lambda i: (0, 0))],
        out_specs=spec,
        out_shape=jax.ShapeDtypeStruct((m, d), F32),
        compiler_params=_params("arbitrary"),
        name="final_norm",
    )(h1, peer, g.reshape(1, d))


def _alibi_slopes():
    h = jnp.arange(1, N_ATTN_HEADS + 1, dtype=F32)
    return jnp.exp2(-ALIBI_MAX_EXP * h / N_ATTN_HEADS)


def kernel(x, norm_mix_g, w_in, gate_b, conv_dw_w, conv_dw_b, conv_ln_g, conv_ln_b,
           conv_pw_w, attn_o_w, w_out, norm_ffn_g, peer_w_q, peer_sub_keys_1,
           peer_sub_keys_2, peer_u, peer_v, norm_final_g):
    b, s, d = x.shape
    t = b * s
    depth = w_in.shape[0]
    o_glu = 2 * CONV_CH
    o_v = o_glu + 3 * ATTN_WIDTH
    slopes = _alibi_slopes()
    h = x.reshape(t, d)
    for l in range(depth):
        w_in_b = w_in[l].astype(BF16)
        xn = _rmsnorm(h, norm_mix_g[l], BF16)
        glu = _proj_glu(xn, w_in_b)
        qkv = _proj_plain(xn, w_in_b, o_glu, 3 * ATTN_WIDTH, BF16, name="proj_qkv")
        gates = _proj_gate(xn, w_in_b, gate_b[l], o_v)
        hc = _conformer_conv(glu.reshape(b, s, CONV_CH), conv_dw_w[l], conv_dw_b[l],
                             conv_ln_g[l], conv_ln_b[l]).reshape(t, CONV_CH)
        qkv3 = qkv.reshape(b, s, 3 * ATTN_WIDTH)
        outs, lses = [], []
        for gi, (window, dilation) in enumerate(DIL_GROUPS):
            sl = slopes[gi * HEADS_PER_GROUP:(gi + 1) * HEADS_PER_GROUP]
            o, lse = _attn_group(qkv3, sl, gi, window, dilation)
            outs.append(o)
            lses.append(lse)
        am = _mix_groups(outs, lses)
        mixed = _gated_merge(hc, am, conv_pw_w[l].astype(BF16), attn_o_w[l].astype(BF16), gates)
        h1, hn = _out_proj(mixed, w_out[l].astype(BF16), h, norm_ffn_g[l])
        q = _proj_plain(hn, peer_w_q[l].astype(BF16), 0, PEER_HEADS * PEER_QDIM, F32,
                        name="peer_query")
        p1, e1, p2, e2, tau = _peer_scores(q, peer_sub_keys_1[l], peer_sub_keys_2[l])
        peer = _peer_dense(hn, peer_u[l].astype(BF16).T, peer_v[l].astype(BF16),
                           p1, e1, p2, e2, tau)
        if l == depth - 1:
            return _final_norm(h1, peer, norm_final_g).reshape(b, s, d)
        h = _residual_add(h1, peer)
    return h.reshape(b, s, d)


def _add_body(a_ref, b_ref, o_ref):
    o_ref[...] = a_ref[...] + b_ref[...]


def _residual_add(a, b_, tm=256):
    m, d = a.shape
    spec = pl.BlockSpec((tm, d), lambda i: (i, 0))
    return pl.pallas_call(
        _add_body, grid=(m // tm,), in_specs=[spec, spec], out_specs=spec,
        out_shape=jax.ShapeDtypeStruct((m, d), F32),
        compiler_params=_params("arbitrary"), name="residual_add",
    )(a, b_)
```

```python
import functools

import jax
import jax.numpy as jnp
from jax import lax
from jax.experimental import pallas as pl
from jax.experimental.pallas import tpu as pltpu

F32 = jnp.float32
BF16 = jnp.bfloat16

D_MODEL = 4096
CONV_CH = D_MODEL
CONV_WIDTH = 31
HEAD_DIM = 128
DIL_GROUPS = ((128, 1), (512, 4), (2048, 16))
HEADS_PER_GROUP = 8
N_ATTN_HEADS = HEADS_PER_GROUP * len(DIL_GROUPS)
ATTN_WIDTH = N_ATTN_HEADS * HEAD_DIM
ATTN_MIX_WIDTH = HEADS_PER_GROUP * HEAD_DIM
ATTN_BLOCK = 128
ALIBI_MAX_EXP = 8.0
PEER_HEADS = 8
PEER_NKEYS = 128
PEER_EXPERTS = PEER_NKEYS * PEER_NKEYS
PEER_QDIM = 256
PEER_TOPK = 16
RMS_EPS = 1e-6
LN_EPS = 1e-5

V7X_LANES = 128
V7X_SUBLANES = 8
V7X_VMEM_LIMIT_BYTES = 56 * 1024 * 1024

NEG_BIG = -1e30
INV_SQRT2 = 0.7071067811865476


def _params(*sem, flags=None):
    return pltpu.CompilerParams(
        dimension_semantics=sem if sem else None,
        vmem_limit_bytes=V7X_VMEM_LIMIT_BYTES,
        flags=flags)


def _rmsnorm_body(x_ref, g_ref, o_ref):
    x = x_ref[...]
    ms = jnp.mean(x * x, axis=-1, keepdims=True)
    o_ref[...] = (x * lax.rsqrt(ms + RMS_EPS) * g_ref[...]).astype(o_ref.dtype)


def _rmsnorm(x2d, g, out_dtype, tm=256):
    m, d = x2d.shape
    return pl.pallas_call(
        _rmsnorm_body,
        grid=(m // tm,),
        in_specs=[pl.BlockSpec((tm, d), lambda i: (i, 0)),
                  pl.BlockSpec((1, d), lambda i: (0, 0))],
        out_specs=pl.BlockSpec((tm, d), lambda i: (i, 0)),
        out_shape=jax.ShapeDtypeStruct((m, d), out_dtype),
        compiler_params=_params("arbitrary"),
        name="rmsnorm",
    )(x2d, g.reshape(1, d))


def _w_spec(k, tn, off):
    return pl.BlockSpec((k, tn), lambda j, i: (0, j + off), pipeline_mode=pl.Buffered(1))


def _cast_weight_once(w_ref, ws_ref):
    @pl.when(pl.program_id(1) == 0)
    def _():
        ws_ref[...] = w_ref[...].astype(ws_ref.dtype)


def _glu_body(a_ref, wa_ref, wb_ref, o_ref, was_ref, wbs_ref):
    _cast_weight_once(wa_ref, was_ref)
    _cast_weight_once(wb_ref, wbs_ref)
    a = a_ref[...]
    ya = jnp.dot(a, was_ref[...], preferred_element_type=F32)
    yb = jnp.dot(a, wbs_ref[...], preferred_element_type=F32)
    o_ref[...] = (ya * jax.nn.sigmoid(yb)).astype(o_ref.dtype)


def _plain_body(a_ref, w_ref, o_ref, ws_ref):
    _cast_weight_once(w_ref, ws_ref)
    o_ref[...] = jnp.dot(a_ref[...], ws_ref[...],
                         preferred_element_type=F32).astype(o_ref.dtype)


def _gate_body(a_ref, w_ref, b_ref, o_ref, ws_ref):
    _cast_weight_once(w_ref, ws_ref)
    y = jnp.dot(a_ref[...], ws_ref[...], preferred_element_type=F32)
    o_ref[...] = jax.nn.sigmoid(y + b_ref[...]).astype(o_ref.dtype)


def _qkv_body(a_ref, w_ref, o_ref, ws_ref, y_ref, *, dilation):
    _cast_weight_once(w_ref, ws_ref)
    y = jnp.dot(a_ref[...], ws_ref[...], preferred_element_type=F32)
    if dilation == 1:
        o_ref[0, 0] = y.astype(o_ref.dtype)
        return
    tm, n = y.shape
    rows = tm // dilation
    for k in range(n // V7X_LANES):
        y_ref[k] = y[:, k * V7X_LANES:(k + 1) * V7X_LANES]
    for r in range(dilation):
        for k in range(n // V7X_LANES):
            o_ref[0, r, :, k * V7X_LANES:(k + 1) * V7X_LANES] = (
                y_ref[k, pl.ds(r, rows, stride=dilation), :].astype(o_ref.dtype))


def _proj_glu(xn, w, tm=512, tn=512):
    m, k = xn.shape
    n = CONV_CH
    return pl.pallas_call(
        _glu_body,
        grid=(n // tn, m // tm),
        in_specs=[pl.BlockSpec((tm, k), lambda j, i: (i, 0)),
                  _w_spec(k, tn, 0), _w_spec(k, tn, n // tn)],
        out_specs=pl.BlockSpec((tm, tn), lambda j, i: (i, j)),
        out_shape=jax.ShapeDtypeStruct((m, n), F32),
        scratch_shapes=[pltpu.VMEM((k, tn), BF16), pltpu.VMEM((k, tn), BF16)],
        compiler_params=_params("arbitrary", "arbitrary"),
        name="proj_glu",
    )(xn, w, w)


def _proj_plain(a, w, col0, n, out_dtype, tm=512, tn=1024, name="proj"):
    m, k = a.shape
    return pl.pallas_call(
        _plain_body,
        grid=(n // tn, m // tm),
        in_specs=[pl.BlockSpec((tm, k), lambda j, i: (i, 0)),
                  _w_spec(k, tn, col0 // tn)],
        out_specs=pl.BlockSpec((tm, tn), lambda j, i: (i, j)),
        out_shape=jax.ShapeDtypeStruct((m, n), out_dtype),
        scratch_shapes=[pltpu.VMEM((k, tn), BF16)],
        compiler_params=_params("arbitrary", "arbitrary"),
        name=name,
    )(a, w)


def _proj_gate(xn, w, bias, col0, tm=512, tn=1024):
    m, k = xn.shape
    n = bias.shape[-1]
    return pl.pallas_call(
        _gate_body,
        grid=(n // tn, m // tm),
        in_specs=[pl.BlockSpec((tm, k), lambda j, i: (i, 0)),
                  _w_spec(k, tn, col0 // tn),
                  pl.BlockSpec((1, tn), lambda j, i: (0, j))],
        out_specs=pl.BlockSpec((tm, tn), lambda j, i: (i, j)),
        out_shape=jax.ShapeDtypeStruct((m, n), BF16),
        scratch_shapes=[pltpu.VMEM((k, tn), BF16)],
        compiler_params=_params("arbitrary", "arbitrary"),
        name="proj_gate",
    )(xn, w, bias.reshape(1, n))


def _proj_qkv_group(xn, w, col0, gi, dilation, batch, seq, tm=512):
    m, k = xn.shape
    gw = ATTN_MIX_WIDTH
    nbt = seq // tm
    n_groups = len(DIL_GROUPS)
    off = col0 // gw
    w_spec = pl.BlockSpec((k, gw), lambda j, i: (0, off + j * n_groups + gi),
                          pipeline_mode=pl.Buffered(1))
    return pl.pallas_call(
        functools.partial(_qkv_body, dilation=dilation),
        grid=(3, m // tm),
        in_specs=[pl.BlockSpec((tm, k), lambda j, i: (i, 0)), w_spec],
        out_specs=pl.BlockSpec((1, dilation, tm // dilation, gw),
                               lambda j, i: (i // nbt, 0, i % nbt, j)),
        out_shape=jax.ShapeDtypeStruct((batch, dilation, seq // dilation, 3 * gw), BF16),
        scratch_shapes=[pltpu.VMEM((k, gw), BF16),
                        pltpu.VMEM((gw // V7X_LANES, tm, V7X_LANES), F32)],
        compiler_params=_params("arbitrary", "arbitrary"),
        name=f"proj_qkv_g{gi}",
    )(xn, w)


CONV_HALO = 32
CONV_ROWS = 64


def _conv_body(h_ref, halo_ref, w_ref, b_ref, g_ref, be_ref, o_ref, hp_ref, cv_ref,
               *, ts, cw):
    i = pl.program_id(1)
    c = pl.program_id(2)
    nc = pl.num_programs(2)
    for k in range(cw // V7X_LANES):
        cs = slice(k * V7X_LANES, (k + 1) * V7X_LANES)
        halo = halo_ref[0, :, cs]
        hp_ref[k, 0:CONV_HALO, :] = jnp.where(i > 0, halo, jnp.zeros_like(halo))
        hp_ref[k, CONV_HALO:, :] = h_ref[0, :, cs]
    base = CONV_HALO - (CONV_WIDTH - 1)
    for k in range(cw // V7X_LANES):
        cs = slice(k * V7X_LANES, (k + 1) * V7X_LANES)
        for r0 in range(0, ts, CONV_ROWS):
            acc = jnp.zeros((CONV_ROWS, V7X_LANES), F32) + b_ref[:, cs]
            for w in range(CONV_WIDTH):
                acc = acc + hp_ref[k, r0 + base + w:r0 + base + w + CONV_ROWS, :] * w_ref[w:w + 1, cs]
            cv_ref[c, r0:r0 + CONV_ROWS, cs] = acc

    @pl.when(c == nc - 1)
    def _():
        n_ch = cv_ref.shape[0]
        tot = jnp.zeros((ts, 1), F32)
        for k in range(n_ch):
            tot = tot + jnp.sum(cv_ref[k], axis=-1, keepdims=True)
        mu = tot * (1.0 / (n_ch * cw))
        var = jnp.zeros((ts, 1), F32)
        for k in range(n_ch):
            d = cv_ref[k] - mu
            var = var + jnp.sum(d * d, axis=-1, keepdims=True)
        rstd = lax.rsqrt(var * (1.0 / (n_ch * cw)) + LN_EPS)
        for k in range(n_ch):
            y = (cv_ref[k] - mu) * rstd * g_ref[:, k * cw:(k + 1) * cw] + be_ref[:, k * cw:(k + 1) * cw]
            o_ref[0, :, k * cw:(k + 1) * cw] = (y * jax.nn.sigmoid(y)).astype(o_ref.dtype)


def _conformer_conv(h3, dw_w, dw_b, ln_g, ln_b, ts=512, cw=512):
    b, s, ch = h3.shape
    hb = ts // CONV_HALO
    body = functools.partial(_conv_body, ts=ts, cw=cw)
    return pl.pallas_call(
        body,
        grid=(b, s // ts, ch // cw),
        in_specs=[
            pl.BlockSpec((1, ts, cw), lambda bb, i, c: (bb, i, c)),
            pl.BlockSpec((1, CONV_HALO, cw), lambda bb, i, c: (bb, jnp.maximum(i * hb - 1, 0), c)),
            pl.BlockSpec((CONV_WIDTH, cw), lambda bb, i, c: (0, c)),
            pl.BlockSpec((1, cw), lambda bb, i, c: (0, c)),
            pl.BlockSpec((1, ch), lambda bb, i, c: (0, 0)),
            pl.BlockSpec((1, ch), lambda bb, i, c: (0, 0)),
        ],
        out_specs=pl.BlockSpec((1, ts, ch), lambda bb, i, c: (bb, i, 0)),
        out_shape=jax.ShapeDtypeStruct((b, s, ch), BF16),
        scratch_shapes=[pltpu.VMEM((cw // V7X_LANES, ts + CONV_HALO, V7X_LANES), F32),
                        pltpu.VMEM((ch // cw, ts, cw), F32)],
        compiler_params=_params("arbitrary", "arbitrary", "arbitrary"),
        name="conformer_conv",
    )(h3, h3, dw_w, dw_b.reshape(1, ch), ln_g.reshape(1, ch), ln_b.reshape(1, ch))


def _attn_block(q_ref, k_ref, v_ref, slope, r, n, o_tok, l_tok, gi, *, dilation, win, nb):
    blk = ATTN_BLOCK
    qi = lax.broadcasted_iota(jnp.int32, (blk, blk), 0)
    kj = lax.broadcasted_iota(jnp.int32, (blk, blk), 1)
    scale = HEAD_DIM ** -0.5
    nt = (((1,), (1,)), ((), ()))
    row0 = pl.multiple_of(n * blk, blk)
    q = q_ref[0, r, pl.ds(row0, blk), :]
    delta_c = qi - kj
    valid_c = delta_c >= 0
    if win < blk:
        valid_c = valid_c & (delta_c <= win)
    s_c = lax.dot_general(q, k_ref[0, r, pl.ds(row0, blk), :], nt,
                          preferred_element_type=F32) * scale
    s_c = jnp.where(valid_c, s_c + (-slope) * (delta_c * dilation).astype(F32), NEG_BIG)
    m = jnp.max(s_c, axis=-1, keepdims=True)
    if nb > 1:
        prow0 = pl.multiple_of(jnp.maximum(n - 1, 0) * blk, blk)
        delta_p = qi + blk - kj
        s_p = lax.dot_general(q, k_ref[0, r, pl.ds(prow0, blk), :], nt,
                              preferred_element_type=F32) * scale
        s_p = jnp.where(delta_p <= win, s_p + (-slope) * (delta_p * dilation).astype(F32), NEG_BIG)
        s_p = jnp.where(n > 0, s_p, NEG_BIG)
        m = jnp.maximum(m, jnp.max(s_p, axis=-1, keepdims=True))
    p_c = jnp.exp(s_c - m)
    den = jnp.sum(p_c, axis=-1, keepdims=True)
    o = jnp.dot(p_c.astype(BF16), v_ref[0, r, pl.ds(row0, blk), :], preferred_element_type=F32)
    if nb > 1:
        p_p = jnp.exp(s_p - m)
        den = den + jnp.sum(p_p, axis=-1, keepdims=True)
        o = o + jnp.dot(p_p.astype(BF16), v_ref[0, r, pl.ds(prow0, blk), :],
                        preferred_element_type=F32)
    start = n * (blk * dilation) + r
    if dilation == 1:
        rows = pl.ds(pl.multiple_of(start, blk), blk)
    else:
        rows = pl.ds(start, blk, stride=dilation)
    o_tok[gi, rows, :] = o / den
    l_tok[gi, rows, :] = jnp.broadcast_to(m + jnp.log(den), (blk, HEAD_DIM))


def _attn_fused_body(*refs, geoms):
    n_g = len(geoms)
    qkv_refs = refs[:3 * n_g]
    sl_ref, out_ref, o_tok, l_tok = refs[3 * n_g:]
    for gi, (dilation, win, nb) in enumerate(geoms):
        q_ref, k_ref, v_ref = qkv_refs[3 * gi:3 * gi + 3]
        slope = sl_ref[0, gi:gi + 1, :]

        def step(idx, carry, q_ref=q_ref, k_ref=k_ref, v_ref=v_ref, slope=slope, gi=gi,
                 dilation=dilation, win=win, nb=nb):
            _attn_block(q_ref, k_ref, v_ref, slope, idx // nb, idx % nb, o_tok, l_tok, gi,
                        dilation=dilation, win=win, nb=nb)
            return carry

        lax.fori_loop(0, dilation * nb, step, 0)
    lse = [l_tok[gi] for gi in range(n_g)]
    m = functools.reduce(jnp.maximum, lse)
    es = [jnp.exp(a - m) for a in lse]
    den = functools.reduce(lambda a, b: a + b, es)
    acc = es[0] * o_tok[0]
    for gi in range(1, n_g):
        acc = acc + es[gi] * o_tok[gi]
    out_ref[0] = (acc / den).astype(out_ref.dtype)


def _attention(qkv_groups, slopes, batch, seq):
    gw = ATTN_MIX_WIDTH
    hpg = HEADS_PER_GROUP
    geoms, in_specs, args = [], [], []
    for (window, dilation), arr in zip(DIL_GROUPS, qkv_groups):
        l = seq // dilation
        win = window // dilation
        assert l % ATTN_BLOCK == 0 and win <= ATTN_BLOCK
        geoms.append((dilation, win, l // ATTN_BLOCK))
        for which in range(3):
            in_specs.append(pl.BlockSpec((1, dilation, l, HEAD_DIM),
                                         lambda b, hs, which=which: (b, 0, 0, which * hpg + hs)))
            args.append(arr)
    n_g = len(geoms)
    sl = jnp.broadcast_to(slopes.reshape(n_g, hpg).T[:, :, None], (hpg, n_g, HEAD_DIM))
    in_specs.append(pl.BlockSpec((1, n_g, HEAD_DIM), lambda b, hs: (hs, 0, 0)))
    out = pl.pallas_call(
        functools.partial(_attn_fused_body, geoms=tuple(geoms)),
        grid=(batch, hpg),
        in_specs=in_specs,
        out_specs=pl.BlockSpec((1, seq, HEAD_DIM), lambda b, hs: (b, 0, hs)),
        out_shape=jax.ShapeDtypeStruct((batch, seq, gw), BF16),
        scratch_shapes=[pltpu.VMEM((n_g, seq, HEAD_DIM), F32),
                        pltpu.VMEM((n_g, seq, HEAD_DIM), F32)],
        compiler_params=_params("arbitrary", "arbitrary"),
        name="dilated_attention",
    )(*args, sl)
    return out.reshape(batch * seq, gw)


def _merge_body(hc_ref, am_ref, pw_ref, ow_ref, g0_ref, g1_ref, o_ref, pws_ref, ows_ref):
    _cast_weight_once(pw_ref, pws_ref)
    _cast_weight_once(ow_ref, ows_ref)
    c = jnp.dot(hc_ref[...], pws_ref[...], preferred_element_type=F32)
    a = jnp.dot(am_ref[...], ows_ref[...], preferred_element_type=F32)
    o_ref[...] = (g0_ref[...].astype(F32) * c + g1_ref[...].astype(F32) * a).astype(o_ref.dtype)


def _gated_merge(hc, am, pw, ow, gates, tm=512, tn=512):
    m, k1 = hc.shape
    k2 = am.shape[1]
    n = pw.shape[1]
    off = n // tn
    return pl.pallas_call(
        _merge_body,
        grid=(n // tn, m // tm),
        in_specs=[pl.BlockSpec((tm, k1), lambda j, i: (i, 0)),
                  pl.BlockSpec((tm, k2), lambda j, i: (i, 0)),
                  _w_spec(k1, tn, 0), _w_spec(k2, tn, 0),
                  pl.BlockSpec((tm, tn), lambda j, i: (i, j)),
                  pl.BlockSpec((tm, tn), lambda j, i: (i, j + off))],
        out_specs=pl.BlockSpec((tm, tn), lambda j, i: (i, j)),
        out_shape=jax.ShapeDtypeStruct((m, n), BF16),
        scratch_shapes=[pltpu.VMEM((k1, tn), BF16), pltpu.VMEM((k2, tn), BF16)],
        compiler_params=_params("arbitrary", "arbitrary"),
        name="gated_merge",
    )(hc, am, pw, ow, gates, gates)


def _outproj_body(a_ref, w_ref, x_ref, g_ref, h_ref, hn_ref, row_ref, *, tn):
    j = pl.program_id(1)
    nj = pl.num_programs(1)
    y = x_ref[...] + jnp.dot(a_ref[...], w_ref[...], preferred_element_type=F32)
    h_ref[...] = y
    row_ref[j] = y

    @pl.when(j == nj - 1)
    def _():
        n_chunks = row_ref.shape[0]
        ss = jnp.zeros((row_ref.shape[1], 1), F32)
        for k in range(n_chunks):
            v = row_ref[k]
            ss = ss + jnp.sum(v * v, axis=-1, keepdims=True)
        r = lax.rsqrt(ss * (1.0 / (n_chunks * tn)) + RMS_EPS)
        for k in range(n_chunks):
            hn_ref[:, k * tn:(k + 1) * tn] = (
                row_ref[k] * r * g_ref[:, k * tn:(k + 1) * tn]).astype(hn_ref.dtype)


def _out_proj(mixed, w, x2d, g, tm=512, tn=512):
    m, k = mixed.shape
    n = w.shape[1]
    return pl.pallas_call(
        functools.partial(_outproj_body, tn=tn),
        grid=(m // tm, n // tn),
        in_specs=[pl.BlockSpec((tm, k), lambda i, j: (i, 0)),
                  pl.BlockSpec((k, tn), lambda i, j: (0, j)),
                  pl.BlockSpec((tm, tn), lambda i, j: (i, j)),
                  pl.BlockSpec((1, n), lambda i, j: (0, 0))],
        out_specs=[pl.BlockSpec((tm, tn), lambda i, j: (i, j)),
                   pl.BlockSpec((tm, n), lambda i, j: (i, 0))],
        out_shape=[jax.ShapeDtypeStruct((m, n), F32),
                   jax.ShapeDtypeStruct((m, n), BF16)],
        scratch_shapes=[pltpu.VMEM((n // tn, tm, tn), F32)],
        compiler_params=_params("arbitrary", "arbitrary"),
        name="out_proj",
    )(mixed, w, x2d, g.reshape(1, n))


def _top16_rows(x):
    rid = lax.broadcasted_iota(jnp.int32, (PEER_TOPK, x.shape[1]), 0)
    vals = jnp.zeros((PEER_TOPK, x.shape[1]), F32)
    for r in range(PEER_TOPK):
        m = jnp.max(x, axis=0, keepdims=True)
        vals = jnp.where(rid == r, m, vals)
        x = jnp.where(x == m, -jnp.inf, x)
    return vals


def _peer_score_body(q_ref, k1_ref, k2_ref, p1_ref, e1_ref, p2_ref, e2_ref, tau_ref):
    tq = q_ref.shape[0]
    half = PEER_QDIM // 2
    nt = (((1,), (1,)), ((), ()))
    for h in range(PEER_HEADS):
        q1 = q_ref[:, h * PEER_QDIM:h * PEER_QDIM + half]
        q2 = q_ref[:, h * PEER_QDIM + half:(h + 1) * PEER_QDIM]
        p1_ref[h] = lax.dot_general(k1_ref[...], q1, nt, preferred_element_type=F32)
        p2_ref[h] = lax.dot_general(k2_ref[...], q2, nt, preferred_element_type=F32)
        for t0 in range(0, tq, V7X_LANES):
            ls = slice(t0, t0 + V7X_LANES)
            s1 = p1_ref[h, :, ls]
            s2 = p2_ref[h, :, ls]
            v1 = _top16_rows(s1)
            v2 = _top16_rows(s2)
            blocks = [v1[0:1] + v2]
            for a in range(1, 8):
                blocks.append(v1[a:a + 1] + v2[0:8])
            blocks.append(v1[8:16] + v2[0:1])
            cand = jnp.concatenate(blocks, axis=0)
            c = cand
            for _ in range(PEER_TOPK):
                tau = jnp.max(c, axis=0, keepdims=True)
                c = jnp.where(c == tau, -jnp.inf, c)
            m1 = v1[0:1]
            m2 = v2[0:1]
            z = jnp.sum(jnp.where(cand >= tau, jnp.exp(cand - (m1 + m2)), 0.0),
                        axis=0, keepdims=True)
            e1_ref[h, :, ls] = jnp.exp(s1 - m1) / z
            e2_ref[h, :, ls] = jnp.exp(s2 - m2)
            tau_ref[h:h + 1, ls] = tau


def _peer_scores(q, k1, k2, tq=256):
    t = q.shape[0]
    nk = PEER_NKEYS
    big = pl.BlockSpec((PEER_HEADS, nk, tq), lambda i: (0, 0, i))
    big_shape = jax.ShapeDtypeStruct((PEER_HEADS, nk, t), F32)
    return pl.pallas_call(
        _peer_score_body,
        grid=(t // tq,),
        in_specs=[pl.BlockSpec((tq, PEER_HEADS * PEER_QDIM), lambda i: (i, 0)),
                  pl.BlockSpec((nk, PEER_QDIM // 2), lambda i: (0, 0)),
                  pl.BlockSpec((nk, PEER_QDIM // 2), lambda i: (0, 0))],
        out_specs=[big, big, big, big, pl.BlockSpec((PEER_HEADS, tq), lambda i: (0, i))],
        out_shape=[big_shape] * 4 + [jax.ShapeDtypeStruct((PEER_HEADS, t), F32)],
        compiler_params=_params("arbitrary"),
        name="peer_scores",
    )(q, k1, k2)


PEER_SUB_E = 64
PEER_SUB_T = 128


def _peer_dense_body(hn_ref, ut_ref, v_ref, p1_ref, e1_ref, p2_ref, e2_ref, tau_ref,
                     o_ref, wt_ref, wa_ref, act_ref, s1_ref, f1_ref, *, tm, te):
    j = pl.program_id(1)
    n_tiles = pl.num_programs(1) - 1

    @pl.when(j == 0)
    def _():
        o_ref[...] = jnp.zeros_like(o_ref)
        wa_ref[1] = jnp.zeros(wa_ref.shape[1:], wa_ref.dtype)

    jc = jnp.minimum(j, n_tiles - 1)
    n_c = te // PEER_NKEYS
    assert 2 * n_c == V7X_SUBLANES
    base = pl.multiple_of((jc // 2) * V7X_SUBLANES, V7X_SUBLANES)
    upper = (jc % 2) == 1
    for h in range(PEER_HEADS):
        g1 = p1_ref[h, pl.ds(base, V7X_SUBLANES), :]
        ge = e1_ref[h, pl.ds(base, V7X_SUBLANES), :]
        s1_ref[h, 0:n_c, :] = jnp.where(upper, g1[n_c:], g1[:n_c])
        f1_ref[h, 0:n_c, :] = jnp.where(upper, ge[n_c:], ge[:n_c])

    o_ref[...] += jnp.dot(wa_ref[(j + 1) % 2], v_ref[...], preferred_element_type=F32)

    s = jnp.dot(hn_ref[...], ut_ref[...], preferred_element_type=F32)
    act_ref[...] = 0.5 * s * (1.0 + lax.erf(s * INV_SQRT2))

    for t0 in range(0, tm, PEER_SUB_T):
        ls = slice(t0, t0 + PEER_SUB_T)
        for cc in range(n_c):
            for r0 in range(0, PEER_NKEYS, PEER_SUB_E):
                rs = slice(r0, r0 + PEER_SUB_E)
                w = jnp.zeros((PEER_SUB_E, PEER_SUB_T), F32)
                for h in range(PEER_HEADS):
                    tot = p2_ref[h, rs, ls] + s1_ref[h, cc:cc + 1, ls]
                    w = w + jnp.where(tot >= tau_ref[h:h + 1, ls],
                                      e2_ref[h, rs, ls] * f1_ref[h, cc:cc + 1, ls], 0.0)
                wt_ref[cc * PEER_NKEYS + r0:cc * PEER_NKEYS + r0 + PEER_SUB_E, ls] = w
        w_te = wt_ref[:, ls].T
        wa_ref[j % 2, ls, :] = (w_te * act_ref[ls, :]).astype(BF16)


def _peer_dense(hn, ut, v, p1, e1, p2, e2, tau, tm=512, te=512):
    t, d = hn.shape
    n_tiles = ut.shape[1] // te
    once = pl.Buffered(1)
    big = pl.BlockSpec((PEER_HEADS, PEER_NKEYS, tm), lambda i, j: (0, 0, i), pipeline_mode=once)
    return pl.pallas_call(
        functools.partial(_peer_dense_body, tm=tm, te=te),
        grid=(t // tm, n_tiles + 1),
        in_specs=[pl.BlockSpec((tm, d), lambda i, j: (i, 0), pipeline_mode=once),
                  pl.BlockSpec((d, te), lambda i, j: (0, jnp.minimum(j, n_tiles - 1))),
                  pl.BlockSpec((te, d), lambda i, j: (jnp.maximum(j - 1, 0), 0)),
                  big, big, big, big,
                  pl.BlockSpec((PEER_HEADS, tm), lambda i, j: (0, i), pipeline_mode=once)],
        out_specs=pl.BlockSpec((tm, d), lambda i, j: (i, 0)),
        out_shape=jax.ShapeDtypeStruct((t, d), F32),
        scratch_shapes=[pltpu.VMEM((te, tm), F32), pltpu.VMEM((2, tm, te), BF16),
                        pltpu.VMEM((tm, te), F32),
                        pltpu.VMEM((PEER_HEADS, V7X_SUBLANES, tm), F32),
                        pltpu.VMEM((PEER_HEADS, V7X_SUBLANES, tm), F32)],
        compiler_params=_params("arbitrary", "arbitrary"),
        name="peer_dense",
    )(hn, ut, v, p1, e1, p2, e2, tau)


def _final_body(h_ref, p_ref, g_ref, o_ref):
    x = h_ref[...] + p_ref[...]
    ms = jnp.mean(x * x, axis=-1, keepdims=True)
    o_ref[...] = x * lax.rsqrt(ms + RMS_EPS) * g_ref[...]


def _final_norm(h1, peer, g, tm=256):
    m, d = h1.shape
    spec = pl.BlockSpec((tm, d), lambda i: (i, 0))
    return pl.pallas_call(
        _final_body,
        grid=(m // tm,),
        in_specs=[spec, spec, pl.BlockSpec((1, d), lambda i: (0, 0))],
        out_specs=spec,
        out_shape=jax.ShapeDtypeStruct((m, d), F32),
        compiler_params=_params("arbitrary"),
        name="final_norm",
    )(h1, peer, g.reshape(1, d))


def _alibi_slopes():
    h = jnp.arange(1, N_ATTN_HEADS + 1, dtype=F32)
    return jnp.exp2(-ALIBI_MAX_EXP * h / N_ATTN_HEADS)


def kernel(x, norm_mix_g, w_in, gate_b, conv_dw_w, conv_dw_b, conv_ln_g, conv_ln_b,
           conv_pw_w, attn_o_w, w_out, norm_ffn_g, peer_w_q, peer_sub_keys_1,
           peer_sub_keys_2, peer_u, peer_v, norm_final_g):
    b, s, d = x.shape
    t = b * s
    depth = w_in.shape[0]
    o_glu = 2 * CONV_CH
    o_v = o_glu + 3 * ATTN_WIDTH
    slopes = _alibi_slopes()
    h = x.reshape(t, d)
    for l in range(depth):
        xn = _rmsnorm(h, norm_mix_g[l], BF16)
        glu = _proj_glu(xn, w_in[l])
        qkv_groups = [_proj_qkv_group(xn, w_in[l], o_glu, gi, dilation, b, s)
                      for gi, (_, dilation) in enumerate(DIL_GROUPS)]
        gates = _proj_gate(xn, w_in[l], gate_b[l], o_v)
        hc = _conformer_conv(glu.reshape(b, s, CONV_CH), conv_dw_w[l], conv_dw_b[l],
                             conv_ln_g[l], conv_ln_b[l]).reshape(t, CONV_CH)
        am = _attention(qkv_groups, slopes, b, s)
        mixed = _gated_merge(hc, am, conv_pw_w[l], attn_o_w[l], gates)
        h1, hn = _out_proj(mixed, w_out[l].astype(BF16), h, norm_ffn_g[l])
        q = _proj_plain(hn, peer_w_q[l], 0, PEER_HEADS * PEER_QDIM, F32, name="peer_query")
        p1, e1, p2, e2, tau = _peer_scores(q, peer_sub_keys_1[l], peer_sub_keys_2[l])
        peer = _peer_dense(hn, peer_u[l].astype(BF16).T, peer_v[l].astype(BF16),
                           p1, e1, p2, e2, tau)
        if l == depth - 1:
            return _final_norm(h1, peer, norm_final_g).reshape(b, s, d)
        h = _residual_add(h1, peer)
    return h.reshape(b, s, d)


def _add_body(a_ref, b_ref, o_ref):
    o_ref[...] = a_ref[...] + b_ref[...]


def _residual_add(a, b_, tm=256):
    m, d = a.shape
    spec = pl.BlockSpec((tm, d), lambda i: (i, 0))
    return pl.pallas_call(
        _add_body, grid=(m // tm,), in_specs=[spec, spec], out_specs=spec,
        out_shape=jax.ShapeDtypeStruct((m, d), F32),
        compiler_params=_params("arbitrary"), name="residual_add",
    )(a, b_)
```
